```python
import math
import jax, jax.numpy as jnp
from jax import lax
import numpy as np

D_MODEL = 1024
BATCH = 8
SEQ = 2048
DEPTH = 4
DEC_BATCH = 32
DEC_SEQ = 4
PAST_LEN = 8192
PAGE_SIZE = 128

N_EVEN = (DEPTH + 1) // 2
N_ODD = DEPTH // 2
HD_A = 64
D_A = D_MODEL // 2
H_A = D_A // HD_A
MOBA_BLOCK = 256
MOBA_TOPK = 3
Q_CHUNK = 16
N_BUCKETS = 32
MAX_DISTANCE = 128
C_CONV = D_MODEL // 2
CONV_W = 31
D_IN_EVEN = 3 * D_A + 2 * C_CONV
D_MIX_EVEN = D_A + C_CONV
D_GM = D_MODEL
GM_CHUNK = 128
G_C = 8
CG = D_GM // G_C
N_MEM = 256
H_X = 4
HD_X = D_MODEL // H_X
N_EXPERTS = 16
N_GROUPS = 4
EPG = N_EXPERTS // N_GROUPS
TOP_K = 2
D_FF = 512
ALPHA = (2 * DEPTH) ** 0.25
BETA = (8 * DEPTH) ** -0.25
LN_EPS = 1e-5

kernel_name = 'hybrid_moba_conv_gmlp_moe_decoder_step'


def layer_norm(x, g, b):
    xf = x.astype(jnp.float32)
    mu = jnp.mean(xf, -1, keepdims=True)
    var = jnp.mean(jnp.square(xf - mu), -1, keepdims=True)
    return ((xf - mu) * lax.rsqrt(var + LN_EPS) * g.astype(jnp.float32) + b.astype(jnp.float32)).astype(x.dtype)


def t5_bucket(rel):
    n = jnp.maximum(rel, 0)
    max_exact = N_BUCKETS // 2
    nf = jnp.maximum(n, 1).astype(jnp.float32)
    large = max_exact + (jnp.log(nf / max_exact) / math.log(MAX_DISTANCE / max_exact)
                         * (N_BUCKETS - max_exact)).astype(jnp.int32)
    large = jnp.minimum(large, N_BUCKETS - 1)
    return jnp.where(n < max_exact, n, large)


def moba_core(q, kb, vb, kmean, qpos, bias_t):
    B, H, Q, _ = q.shape
    nb = kb.shape[1]
    qblk = qpos // MOBA_BLOCK
    gate = jnp.einsum('bhqd,bnhd->bhqn', q.astype(jnp.float32), kmean)
    fully_past = jnp.arange(nb)[None, :] < qblk[:, None]
    gate = jnp.where(fully_past[None, None], gate, -jnp.inf)
    n_sel = min(MOBA_TOPK, nb)
    _, top_idx = lax.top_k(gate, n_sel)
    top_ok = top_idx < qblk[None, None, :, None]
    own = jnp.broadcast_to(qblk[None, None, :, None], (B, H, Q, 1)).astype(jnp.int32)
    sel = jnp.concatenate([top_idx.astype(jnp.int32), own], -1)
    sel_ok = jnp.concatenate([top_ok, jnp.ones((B, H, Q, 1), bool)], -1)
    bi = jnp.arange(B)[:, None, None, None]
    hi = jnp.arange(H)[None, :, None, None]
    kg = kb[bi, sel, :, hi, :]
    vg = vb[bi, sel, :, hi, :]
    kpos = sel[..., None] * MOBA_BLOCK + jnp.arange(MOBA_BLOCK, dtype=jnp.int32)
    qp = qpos[None, None, :, None, None]
    logits = jnp.einsum('bhqd,bhqskd->bhqsk', q, kg, preferred_element_type=jnp.float32) * (HD_A ** -0.5)
    bias = bias_t[jnp.arange(H)[None, :, None, None, None], t5_bucket(qp - kpos)].astype(jnp.float32)
    mask = sel_ok[..., None] & (kpos <= qp)
    logits = jnp.where(mask, logits + bias, -jnp.inf)
    p = jax.nn.softmax(logits.reshape(B, H, Q, -1), axis=-1).reshape(logits.shape)
    return jnp.einsum('bhqsk,bhqskd->bhqd', p.astype(vb.dtype), vg)


def to_blocks(k):
    B, L = k.shape[:2]
    nb = -(-L // MOBA_BLOCK)
    k = jnp.pad(k, ((0, 0), (0, nb * MOBA_BLOCK - L), (0, 0), (0, 0)))
    return k.reshape(B, nb, MOBA_BLOCK, H_A, HD_A)


def moba_prompt(q, k, v, bias_t):
    B, S = q.shape[:2]
    kb, vb = to_blocks(k), to_blocks(v)
    kmean = jnp.mean(kb, axis=2, dtype=jnp.float32)
    nqc = S // Q_CHUNK
    qc = q.reshape(B, nqc, Q_CHUNK, H_A, HD_A).transpose(1, 0, 3, 2, 4)
    pos = jnp.arange(S, dtype=jnp.int32).reshape(nqc, Q_CHUNK)
    out = lax.map(lambda a: moba_core(a[0], kb, vb, kmean, a[1], bias_t), (qc, pos))
    return out.transpose(1, 0, 3, 2, 4).reshape(B, S, D_A)


def moba_sample(q, k_new, v_new, pool_k, pool_v, page_table, bias_t):
    DB, DS = q.shape[:2]
    past_len = page_table.shape[1] * PAGE_SIZE
    total = past_len + DS
    nb = -(-total // MOBA_BLOCK)

    def assemble(pool, new):
        past = pool[page_table].reshape(DB, past_len, H_A, HD_A).astype(new.dtype)
        pad = jnp.zeros((DB, nb * MOBA_BLOCK - total, H_A, HD_A), new.dtype)
        return jnp.concatenate([past, new, pad], 1).reshape(DB, nb, MOBA_BLOCK, H_A, HD_A)

    kb, vb = assemble(pool_k, k_new), assemble(pool_v, v_new)
    kmean = jnp.mean(kb, axis=2, dtype=jnp.float32)
    qpos = past_len + jnp.arange(DS, dtype=jnp.int32)
    out = moba_core(q.transpose(0, 2, 1, 3), kb, vb, kmean, qpos, bias_t)
    return out.transpose(0, 2, 1, 3).reshape(DB, DS, D_A)


def even_in(x, w_in):
    B, L, _ = x.shape
    h = x @ w_in
    q, k, v, a, g = jnp.split(h, [D_A, 2 * D_A, 3 * D_A, 3 * D_A + C_CONV], axis=-1)
    shp = (B, L, H_A, HD_A)
    return q.reshape(shp), k.reshape(shp), v.reshape(shp), a, g


def conv_branch(a, gate_in, prev, w, b, ln_g, ln_b):
    u = a * jax.nn.sigmoid(gate_in)
    u_ext = jnp.concatenate([prev.astype(u.dtype), u], 1)
    y = lax.conv_general_dilated(u_ext, w[:, None, :], window_strides=(1,), padding='VALID',
                                 dimension_numbers=('NWC', 'WIO', 'NWC'),
                                 feature_group_count=C_CONV) + b
    y = jax.nn.silu(layer_norm(y, ln_g, ln_b))
    return y, u_ext[:, -(CONV_W - 1):]


def odd_in(x, w_in, g, b):
    z = jax.nn.gelu(x @ w_in, approximate=False)
    u, v = jnp.split(z, 2, axis=-1)
    return u, layer_norm(v, g, b)


def spatial_gate(u, v, w_s, b_s):
    B, NC, T, _ = v.shape
    v5 = v.reshape(B, NC, T, G_C, CG)
    mix = jnp.einsum('gts,bnsgc->bntgc', w_s, v5) + b_s.T[None, None, :, :, None]
    return u * mix.reshape(B, NC, T, D_GM)


def mem_kv(mem, w_kv):
    B = mem.shape[0]
    kv = mem @ w_kv
    k, v = jnp.split(kv, 2, axis=-1)
    return k.reshape(B, N_MEM, H_X, HD_X), v.reshape(B, N_MEM, H_X, HD_X)


def cross_attn(x, mk, mv, wq, wo):
    B, L, _ = x.shape
    q = (x @ wq).reshape(B, L, H_X, HD_X)
    s = jnp.einsum('blhd,bmhd->bhlm', q, mk.astype(q.dtype), preferred_element_type=jnp.float32) * (HD_X ** -0.5)
    p = jax.nn.softmax(s, axis=-1).astype(x.dtype)
    o = jnp.einsum('bhlm,bmhd->blhd', p, mv.astype(x.dtype)).reshape(B, L, D_MODEL)
    return o @ wo


def moe_tokens(x, w_router, r_bias, wg, wu, wd):
    T = x.shape[0]
    scores = jax.nn.sigmoid(jnp.einsum('td,de->te', x, w_router, preferred_element_type=jnp.float32))
    sel = (scores + r_bias.astype(jnp.float32)).reshape(T, N_GROUPS, EPG)
    grp_score = jnp.sum(lax.top_k(sel, TOP_K)[0], -1)
    best = jnp.argmax(grp_score, -1)
    in_grp = sel[jnp.arange(T), best]
    _, loc = lax.top_k(in_grp, TOP_K)
    idx = best[:, None] * EPG + loc
    w = jnp.take_along_axis(scores, idx, -1)
    w = w / jnp.sum(w, -1, keepdims=True)
    combine = jnp.sum(jax.nn.one_hot(idx, N_EXPERTS, dtype=jnp.float32) * w[..., None], axis=1)
    h = jax.nn.silu(jnp.einsum('td,edf->tef', x, wg)) * jnp.einsum('td,edf->tef', x, wu)
    h = h * combine[..., None].astype(h.dtype)
    return jnp.einsum('tef,efd->td', h, wd)


def moe(x, w_router, r_bias, wg, wu, wd):
    return lax.map(lambda xb: moe_tokens(xb, w_router, r_bias, wg, wu, wd), x)


def setup_inputs(seed: int = 0) -> dict:
    key = jax.random.key(seed)
    ks = jax.random.split(key, 40)
    f32 = jnp.float32
    n_pages = PAST_LEN // PAGE_SIZE
    n_used = DEC_BATCH * n_pages
    n_phys = n_used + n_used // 4

    def nrm(i, shape, scale=1.0):
        return jax.random.normal(ks[i], shape, f32) * scale

    page_table = jax.random.permutation(ks[0], n_phys)[:n_used].reshape(DEC_BATCH, n_pages).astype(jnp.int32)
    col_even = jnp.concatenate([jnp.ones((2 * D_A,), f32), jnp.full((D_A,), BETA, f32),
                                jnp.full((C_CONV,), BETA, f32), jnp.ones((C_CONV,), f32)])
    col_xkv = jnp.concatenate([jnp.ones((D_MODEL,), f32), jnp.full((D_MODEL,), BETA, f32)])
    col_odd = jnp.concatenate([jnp.full((D_GM,), BETA, f32), jnp.ones((D_GM,), f32)])
    return {
        'x_prompt': nrm(1, (BATCH, SEQ, D_MODEL)),
        'x_sample': nrm(2, (DEC_BATCH, DEC_SEQ, D_MODEL)),
        'cache_k': nrm(3, (N_EVEN, n_phys, PAGE_SIZE, H_A, HD_A)),
        'cache_v': nrm(4, (N_EVEN, n_phys, PAGE_SIZE, H_A, HD_A)),
        'state_conv': nrm(5, (N_EVEN, DEC_BATCH, CONV_W - 1, C_CONV), 0.5),
        'cache_mem_k': nrm(6, (DEPTH, DEC_BATCH, N_MEM, H_X, HD_X)),
        'cache_mem_v': nrm(7, (DEPTH, DEC_BATCH, N_MEM, H_X, HD_X)),
        'page_table': page_table,
        'mem_prompt': nrm(8, (BATCH, N_MEM, D_MODEL)),
        'rel_bias': nrm(9, (N_BUCKETS, H_A), 0.2),
        'w_in_even': nrm(10, (N_EVEN, D_MODEL, D_IN_EVEN), D_MODEL ** -0.5) * col_even,
        'conv_w': nrm(11, (N_EVEN, CONV_W, C_CONV), CONV_W ** -0.5),
        'conv_b': nrm(12, (N_EVEN, C_CONV), 0.02),
        'conv_ln_g': 1.0 + nrm(13, (N_EVEN, C_CONV), 0.02),
        'conv_ln_b': nrm(14, (N_EVEN, C_CONV), 0.02),
        'w_out_even': nrm(15, (N_EVEN, D_MIX_EVEN, D_MODEL), BETA * D_MIX_EVEN ** -0.5),
        'w_in_odd': nrm(16, (N_ODD, D_MODEL, 2 * D_GM), D_MODEL ** -0.5) * col_odd,
        'sgu_ln_g': 1.0 + nrm(17, (N_ODD, D_GM), 0.02),
        'sgu_ln_b': nrm(18, (N_ODD, D_GM), 0.02),
        'sgu_w': nrm(19, (N_ODD, G_C, GM_CHUNK, GM_CHUNK), GM_CHUNK ** -0.5),
        'sgu_b': 1.0 + nrm(20, (N_ODD, G_C, GM_CHUNK), 0.02),
        'w_out_odd': nrm(21, (N_ODD, D_GM, D_MODEL), BETA * D_GM ** -0.5),
        'w_xq': nrm(22, (DEPTH, D_MODEL, D_MODEL), D_MODEL ** -0.5),
        'w_xkv': nrm(23, (DEPTH, D_MODEL, 2 * D_MODEL), D_MODEL ** -0.5) * col_xkv,
        'w_xo': nrm(24, (DEPTH, D_MODEL, D_MODEL), BETA * D_MODEL ** -0.5),
        'w_router': nrm(25, (D_MODEL, N_EXPERTS), D_MODEL ** -0.5),
        'router_bias': nrm(26, (N_EXPERTS,), 0.01),
        'w_exp_gate': nrm(27, (DEPTH, N_EXPERTS, D_MODEL, D_FF), D_MODEL ** -0.5),
        'w_exp_up': nrm(28, (DEPTH, N_EXPERTS, D_MODEL, D_FF), BETA * D_MODEL ** -0.5),
        'w_exp_down': nrm(29, (DEPTH, N_EXPERTS, D_FF, D_MODEL), BETA * D_FF ** -0.5),
        'ln_g': 1.0 + nrm(30, (DEPTH, 3, D_MODEL), 0.02),
        'ln_b': nrm(31, (DEPTH, 3, D_MODEL), 0.02),
    }


def reference(x_prompt, x_sample, cache_k, cache_v, state_conv, cache_mem_k, cache_mem_v, page_table,
              mem_prompt, rel_bias, w_in_even, conv_w, conv_b, conv_ln_g, conv_ln_b, w_out_even,
              w_in_odd, sgu_ln_g, sgu_ln_b, sgu_w, sgu_b, w_out_odd, w_xq, w_xkv, w_xo,
              w_router, router_bias, w_exp_gate, w_exp_up, w_exp_down, ln_g, ln_b):
    bias_t = rel_bias.T
    causal = jnp.tril(jnp.ones((GM_CHUNK, GM_CHUNK), dtype=sgu_w.dtype))
    xp, xs = x_prompt, x_sample
    B, S, _ = xp.shape
    DB, DS, _ = xs.shape
    k_p, v_p, k_s, v_s, c_p, c_s, mk_p, mv_p, gv_s = ([] for _ in range(9))
    for l in range(DEPTH):
        if l % 2 == 0:
            e = l // 2
            q, k, v, a, g = even_in(xp, w_in_even[e])
            att = moba_prompt(q, k, v, bias_t)
            cv, cst = conv_branch(a, g, jnp.zeros((B, CONV_W - 1, C_CONV), xp.dtype),
                                  conv_w[e], conv_b[e], conv_ln_g[e], conv_ln_b[e])
            mix_p = jnp.concatenate([att, cv], -1) @ w_out_even[e]
            k_p.append(k); v_p.append(v); c_p.append(cst)
            q, k, v, a, g = even_in(xs, w_in_even[e])
            att = moba_sample(q, k, v, cache_k[e], cache_v[e], page_table, bias_t)
            cv, cst = conv_branch(a, g, state_conv[e], conv_w[e], conv_b[e], conv_ln_g[e], conv_ln_b[e])
            mix_s = jnp.concatenate([att, cv], -1) @ w_out_even[e]
            k_s.append(k); v_s.append(v); c_s.append(cst)
        else:
            o = l // 2
            ws = sgu_w[o] * causal
            u, vv = odd_in(xp, w_in_odd[o], sgu_ln_g[o], sgu_ln_b[o])
            nc = S // GM_CHUNK
            mix_p = spatial_gate(u.reshape(B, nc, GM_CHUNK, D_GM), vv.reshape(B, nc, GM_CHUNK, D_GM),
                                 ws, sgu_b[o]).reshape(B, S, D_GM) @ w_out_odd[o]
            u, vv = odd_in(xs, w_in_odd[o], sgu_ln_g[o], sgu_ln_b[o])
            mix_s = spatial_gate(u[:, None], vv[:, None], ws[:, :DS, :DS],
                                 sgu_b[o][:, :DS]).reshape(DB, DS, D_GM) @ w_out_odd[o]
            gv_s.append(vv)
        xp = layer_norm(ALPHA * xp + mix_p, ln_g[l, 0], ln_b[l, 0])
        xs = layer_norm(ALPHA * xs + mix_s, ln_g[l, 0], ln_b[l, 0])
        mk, mv = mem_kv(mem_prompt, w_xkv[l])
        mk_p.append(mk); mv_p.append(mv)
        xp = layer_norm(ALPHA * xp + cross_attn(xp, mk, mv, w_xq[l], w_xo[l]), ln_g[l, 1], ln_b[l, 1])
        xs = layer_norm(ALPHA * xs + cross_attn(xs, cache_mem_k[l], cache_mem_v[l], w_xq[l], w_xo[l]),
                        ln_g[l, 1], ln_b[l, 1])
        xp = layer_norm(ALPHA * xp + moe(xp, w_router, router_bias, w_exp_gate[l], w_exp_up[l], w_exp_down[l]),
                        ln_g[l, 2], ln_b[l, 2])
        xs = layer_norm(ALPHA * xs + moe(xs, w_router, router_bias, w_exp_gate[l], w_exp_up[l], w_exp_down[l]),
                        ln_g[l, 2], ln_b[l, 2])
    new_k_prompt = jnp.stack(k_p)
    new_v_prompt = jnp.stack(v_p)
    new_k_sample = jnp.stack(k_s)
    new_v_sample = jnp.stack(v_s)
    new_conv_prompt = jnp.stack(c_p)
    new_conv_sample = jnp.stack(c_s)
    new_mem_k_prompt = jnp.stack(mk_p)
    new_mem_v_prompt = jnp.stack(mv_p)
    new_gmlp_v_sample = jnp.stack(gv_s)
    return (xp, xs, new_k_prompt, new_v_prompt, new_k_sample, new_v_sample, new_conv_prompt,
            new_conv_sample, new_mem_k_prompt, new_mem_v_prompt, new_gmlp_v_sample)
```

```python
import functools
import math

import numpy as np
import jax
import jax.numpy as jnp
from jax import lax
from jax.experimental import pallas as pl
from jax.experimental.pallas import tpu as pltpu

F32 = jnp.float32
BF16 = jnp.bfloat16

D_MODEL = 1024
HD_A = 64
H_A = 8
D_A = H_A * HD_A
MOBA_BLOCK = 256
MOBA_TOPK = 3
N_BUCKETS = 32
MAX_DISTANCE = 128
PAGE_SIZE = 128
C_CONV = 512
CONV_W = 31
D_GM = 1024
GM_CHUNK = 128
G_C = 8
H_X = 4
HD_X = D_MODEL // H_X
N_EXPERTS = 16
EPG = 4
DEPTH = 4
ALPHA = (2 * DEPTH) ** 0.25
LN_EPS = 1e-5
NEG_INF = float("-inf")

LANES = 128
SUBLANES = 8
HIST_ROWS = 32
VMEM_LIMIT = 56 * 1024 * 1024


def _cparams(*sem):
    return pltpu.CompilerParams(dimension_semantics=sem, vmem_limit_bytes=VMEM_LIMIT)


def _dot(a, b):
    return jnp.dot(a.astype(BF16), b.astype(BF16), preferred_element_type=F32)


def _dot_nt(a, b):
    return lax.dot_general(a.astype(BF16), b.astype(BF16), (((1,), (1,)), ((), ())),
                           preferred_element_type=F32)


def _split(a):
    hi = a.astype(BF16)
    lo = (a - hi.astype(F32)).astype(BF16)
    return hi, lo


def _dot3(a, b):
    ah, al = _split(a)
    bh, bl = _split(b)
    d = functools.partial(jnp.dot, preferred_element_type=F32)
    return d(ah, bh) + d(ah, bl) + d(al, bh)


def _dot3_nt(a, b):
    ah, al = _split(a)
    bh, bl = _split(b)
    d = functools.partial(lax.dot_general, dimension_numbers=(((1,), (1,)), ((), ())),
                          preferred_element_type=F32)
    return d(ah, bh) + d(ah, bl) + d(al, bh)


def _ln(x, g, b):
    mu = jnp.mean(x, axis=-1, keepdims=True)
    xc = x - mu
    var = jnp.mean(xc * xc, axis=-1, keepdims=True)
    return xc * lax.rsqrt(var + LN_EPS) * g + b


def _sigmoid(x):
    return 1.0 / (1.0 + jnp.exp(-x))


def _even_in_kernel(x_ref, w_ref, q_ref, k_ref, v_ref, ag_ref, km_ref):
    xb = x_ref[...].astype(BF16)
    q_ref[...] = _dot(xb, w_ref[:, 0:D_A])
    k = _dot(xb, w_ref[:, D_A:2 * D_A])
    k_ref[...] = k
    km_ref[0] = jnp.sum(k, axis=0, keepdims=True) * (1.0 / k.shape[0])
    v_ref[...] = _dot(xb, w_ref[:, 2 * D_A:3 * D_A])
    ag_ref[...] = _dot(xb, w_ref[:, 3 * D_A:3 * D_A + 2 * C_CONV])


def even_in(x, w, tm):
    T = x.shape[0]
    n = T // tm
    row = lambda i: (i, 0)
    return pl.pallas_call(
        _even_in_kernel,
        grid=(n,),
        in_specs=[pl.BlockSpec((tm, D_MODEL), row),
                  pl.BlockSpec(w.shape, lambda i: (0, 0))],
        out_specs=[pl.BlockSpec((tm, D_A), row), pl.BlockSpec((tm, D_A), row),
                   pl.BlockSpec((tm, D_A), row), pl.BlockSpec((tm, 2 * C_CONV), row),
                   pl.BlockSpec((1, 1, D_A), lambda i: (i, 0, 0))],
        out_shape=[jax.ShapeDtypeStruct((T, D_A), F32)] * 3
        + [jax.ShapeDtypeStruct((T, 2 * C_CONV), F32), jax.ShapeDtypeStruct((n, 1, D_A), F32)],
        compiler_params=_cparams("parallel"),
        name="even_in",
    )(x, w)


def _moba_prompt_kernel(q_ref, k_ref, v_ref, km_ref, b0_ref, b1_ref, o_ref, m_s, l_s, acc_s):
    i = pl.program_id(2)
    blk = MOBA_BLOCK
    nb = km_ref.shape[1]
    q2 = q_ref[...]
    lane = lax.broadcasted_iota(jnp.int32, (blk, LANES), 1)
    kpos = lax.broadcasted_iota(jnp.int32, (blk, blk), 0)
    qpos = lax.broadcasted_iota(jnp.int32, (blk, blk), 1)
    causal = kpos <= qpos
    blk_id = lax.broadcasted_iota(jnp.int32, (nb, blk), 0)
    past = blk_id < i
    km = km_ref[0]

    qms, sels = [], []
    for hh in range(2):
        qm = jnp.where((lane >= HD_A * hh) & (lane < HD_A * (hh + 1)), q2, 0.0)
        gate = jnp.where(past, _dot3_nt(km, qm), NEG_INF)
        rank = jnp.zeros((nb, blk), jnp.int32)
        for m in range(nb):
            gm = gate[m:m + 1, :]
            rank = rank + ((gm > gate) | ((gm == gate) & (m < blk_id))).astype(jnp.int32)
        sels.append(jnp.where(past & (rank < MOBA_TOPK), 1.0, 0.0))
        qms.append((qm * (HD_A ** -0.5)).astype(BF16))

    def sel_row(hh, j):
        return jnp.sum(jnp.where(blk_id == j, sels[hh], 0.0), axis=0, keepdims=True) > 0.5

    def load_kv(j):
        start = pl.multiple_of(j * blk, blk)
        kj = k_ref[pl.ds(start, blk), :].astype(BF16)
        vt = v_ref[pl.ds(start, blk), :].T.astype(BF16)
        return kj, vt

    kj, vt = load_kv(i)
    for hh in range(2):
        s = jnp.where(causal, _dot_nt(kj, qms[hh]) + b0_ref[hh], NEG_INF)
        m = jnp.max(s, axis=0, keepdims=True)
        p = jnp.exp(s - m)
        m_s[hh] = m
        l_s[hh] = jnp.sum(p, axis=0, keepdims=True)
        acc_s[hh] = jnp.dot(vt, p.astype(BF16), preferred_element_type=F32)

    def update(hh, s, vt):
        m_old = m_s[hh]
        m_new = jnp.maximum(m_old, jnp.max(s, axis=0, keepdims=True))
        alpha = jnp.exp(m_old - m_new)
        p = jnp.exp(s - m_new)
        l_s[hh] = alpha * l_s[hh] + jnp.sum(p, axis=0, keepdims=True)
        acc_s[hh] = alpha * acc_s[hh] + jnp.dot(vt, p.astype(BF16), preferred_element_type=F32)
        m_s[hh] = m_new

    jm = jnp.maximum(i - 1, 0)
    kj, vt = load_kv(jm)
    for hh in range(2):
        s = jnp.where(sel_row(hh, jm), _dot_nt(kj, qms[hh]) + b1_ref[hh], NEG_INF)
        update(hh, s, vt)

    def far(j, carry):
        kj, vt = load_kv(j)
        for hh in range(2):
            s = jnp.where(sel_row(hh, j), _dot_nt(kj, qms[hh]), NEG_INF)
            update(hh, s, vt)
        return carry

    lax.fori_loop(0, jnp.maximum(i - 1, 0), far, 0)

    out_t = jnp.concatenate([acc_s[0][0:HD_A] / l_s[0], acc_s[1][HD_A:2 * HD_A] / l_s[1]], axis=0)
    o_ref[...] = out_t.T


def moba_prompt(q, k, v, km, b0, b1, B, S):
    nb = S // MOBA_BLOCK
    hp = H_A // 2
    return pl.pallas_call(
        _moba_prompt_kernel,
        grid=(B, hp, nb),
        in_specs=[pl.BlockSpec((MOBA_BLOCK, LANES), lambda b, h, i: (b * nb + i, h)),
                  pl.BlockSpec((S, LANES), lambda b, h, i: (b, h)),
                  pl.BlockSpec((S, LANES), lambda b, h, i: (b, h)),
                  pl.BlockSpec((1, nb, LANES), lambda b, h, i: (b, 0, h)),
                  pl.BlockSpec((2, MOBA_BLOCK, MOBA_BLOCK), lambda b, h, i: (h, 0, 0)),
                  pl.BlockSpec((2, MOBA_BLOCK, MOBA_BLOCK), lambda b, h, i: (h, 0, 0))],
        out_specs=pl.BlockSpec((MOBA_BLOCK, LANES), lambda b, h, i: (b * nb + i, h)),
        out_shape=jax.ShapeDtypeStruct((B * S, D_A), F32),
        scratch_shapes=[pltpu.VMEM((2, 1, MOBA_BLOCK), F32), pltpu.VMEM((2, 1, MOBA_BLOCK), F32),
                        pltpu.VMEM((2, LANES, MOBA_BLOCK), F32)],
        compiler_params=_cparams("parallel", "parallel", "arbitrary"),
        name="moba_prompt",
    )(q, k, v, km, b0, b1)


SAMPLE_BLOCKS_PER_STEP = 2


def _moba_sample_kernel(pt_ref, q_ref, kn_ref, vn_ref, *rest, nbs, nb):
    npg = 2 * nbs
    kp = rest[0:npg]
    vp = rest[npg:2 * npg]
    blast_ref, bown_ref, o_ref, g_s, m_s, l_s, acc_s = rest[2 * npg:]
    del pt_ref
    s = pl.program_id(1)
    ds_ = q_ref.shape[1]
    rows = ds_ * H_A
    q = q_ref[0]
    qrep = jnp.concatenate([jnp.broadcast_to(q[i:i + 1, :], (H_A, D_A)) for i in range(ds_)], axis=0)
    row = lax.broadcasted_iota(jnp.int32, (rows, D_A), 0)
    lane = lax.broadcasted_iota(jnp.int32, (rows, D_A), 1)
    hmask = (row % H_A) == (lane // HD_A)
    qm = jnp.where(hmask, qrep, 0.0)
    qmb = (qm * (HD_A ** -0.5)).astype(BF16)
    lane128 = lax.broadcasted_iota(jnp.int32, (rows, LANES), 1)

    @pl.when(s == 0)
    def _():
        g_s[...] = jnp.zeros_like(g_s)
        m_s[...] = jnp.zeros_like(m_s)
        l_s[...] = jnp.zeros_like(l_s)

    for bl in range(nbs):
        n = s * nbs + bl
        kb = jnp.concatenate([kp[2 * bl][0, 0], kp[2 * bl + 1][0, 0]], axis=0)
        vb = jnp.concatenate([vp[2 * bl][0, 0], vp[2 * bl + 1][0, 0]], axis=0)
        kmean = jnp.sum(kb, axis=0, keepdims=True) * (1.0 / MOBA_BLOCK)
        gate = jnp.sum(qm * kmean, axis=1, keepdims=True)
        sc = _dot_nt(qmb, kb)
        sc = sc + jnp.where(n == nb - 1, blast_ref[...], 0.0)
        m = jnp.max(sc, axis=1, keepdims=True)
        p = jnp.exp(sc - m)
        l = jnp.sum(p, axis=1, keepdims=True)
        acc_s[n] = _dot(p, vb)
        here = lane128 == n
        g_s[...] = jnp.where(here, gate, g_s[...])
        m_s[...] = jnp.where(here, m, m_s[...])
        l_s[...] = jnp.where(here, l, l_s[...])

    @pl.when(s == pl.num_programs(1) - 1)
    def _():
        g = g_s[...]
        rank = jnp.zeros((rows, LANES), jnp.int32)
        for mm in range(nb):
            gm = g[:, mm:mm + 1]
            rank = rank + ((gm > g) | ((gm == g) & (mm < lane128))).astype(jnp.int32)
        sel = (lane128 < nb) & (rank < MOBA_TOPK)
        kn = kn_ref[0]
        vn = vn_ref[0]
        bown = bown_ref[...]
        s_own = [jnp.sum(qm * kn[j:j + 1, :], axis=1, keepdims=True) * (HD_A ** -0.5) + bown[:, j:j + 1]
                 for j in range(ds_)]
        m_own = s_own[0]
        for j in range(1, ds_):
            m_own = jnp.maximum(m_own, s_own[j])
        mb = m_s[...]
        m_star = jnp.maximum(jnp.max(jnp.where(sel, mb, NEG_INF), axis=1, keepdims=True), m_own)
        w = jnp.where(sel, jnp.exp(mb - m_star), 0.0)
        den = jnp.sum(w * l_s[...], axis=1, keepdims=True)
        num = jnp.zeros((rows, D_A), F32)
        for j in range(ds_):
            pj = jnp.exp(s_own[j] - m_star)
            den = den + pj
            num = num + pj * vn[j:j + 1, :]
        for n in range(nb):
            num = num + w[:, n:n + 1] * acc_s[n]
        res = jnp.where(hmask, num / den, 0.0)
        o_ref[0] = jnp.concatenate(
            [jnp.sum(res[i * H_A:(i + 1) * H_A], axis=0, keepdims=True) for i in range(ds_)], axis=0)


def moba_sample(q, k_new, v_new, cache_k, cache_v, e, page_table, blast, bown):
    DB, DS, _ = q.shape
    n_pages = page_table.shape[1]
    nb = n_pages * PAGE_SIZE // MOBA_BLOCK
    nbs = SAMPLE_BLOCKS_PER_STEP
    npg = 2 * nbs
    rows = DS * H_A
    small = pl.BlockSpec((1, DS, D_A), lambda b, s, pt: (b, 0, 0))

    def page_spec(i):
        return pl.BlockSpec((1, 1, PAGE_SIZE, D_A), lambda b, s, pt: (e, pt[b, s * npg + i], 0, 0))

    grid_spec = pltpu.PrefetchScalarGridSpec(
        num_scalar_prefetch=1,
        grid=(DB, nb // nbs),
        in_specs=[small, small, small] + [page_spec(i) for i in range(npg)] * 2
        + [pl.BlockSpec(blast.shape, lambda b, s, pt: (0, 0)),
           pl.BlockSpec(bown.shape, lambda b, s, pt: (0, 0))],
        out_specs=small,
        scratch_shapes=[pltpu.VMEM((rows, LANES), F32)] * 3 + [pltpu.VMEM((nb, rows, D_A), F32)],
    )
    return pl.pallas_call(
        functools.partial(_moba_sample_kernel, nbs=nbs, nb=nb),
        grid_spec=grid_spec,
        out_shape=jax.ShapeDtypeStruct((DB, DS, D_A), F32),
        compiler_params=_cparams("parallel", "arbitrary"),
        name="moba_sample",
    )(page_table, q, k_new, v_new, *([cache_k] * npg), *([cache_v] * npg), blast, bown)


CONV_ROWS = 32


def _conv_kernel(ag_ref, prev_ref, w_ref, cb_ref, g_ref, b_ref, cv_ref, st_ref, ubuf, *, carry):
    ts = ag_ref.shape[1]
    pad = HIST_ROWS - (CONV_W - 1)

    @pl.when(pl.program_id(1) == 0)
    def _():
        ubuf[0:HIST_ROWS, :] = prev_ref[0]

    a = ag_ref[0, :, 0:C_CONV]
    gt = ag_ref[0, :, C_CONV:2 * C_CONV]
    ubuf[HIST_ROWS:HIST_ROWS + ts, :] = a * _sigmoid(gt)
    w = w_ref[...]
    rs = min(CONV_ROWS, ts)
    for r in range(0, ts, rs):
        acc = jnp.zeros((rs, C_CONV), F32)
        for k in range(CONV_W):
            acc = acc + w[k:k + 1, :] * ubuf[r + pad + k:r + pad + k + rs, :]
        y = _ln(acc + cb_ref[...], g_ref[...], b_ref[...])
        cv_ref[0, r:r + rs, :] = y * _sigmoid(y)
    st_ref[0] = ubuf[ts + pad:ts + HIST_ROWS, :]
    if carry:
        ubuf[0:HIST_ROWS, :] = ubuf[ts:ts + HIST_ROWS, :]


def conv_branch(ag, prev32, w, cb, g, b, ts):
    B, L, _ = ag.shape
    nt = L // ts
    vec = pl.BlockSpec((1, C_CONV), lambda bb, t: (0, 0))
    return pl.pallas_call(
        functools.partial(_conv_kernel, carry=nt > 1),
        grid=(B, nt),
        in_specs=[pl.BlockSpec((1, ts, 2 * C_CONV), lambda bb, t: (bb, t, 0)),
                  pl.BlockSpec((1, HIST_ROWS, C_CONV), lambda bb, t: (bb, 0, 0)),
                  pl.BlockSpec((CONV_W, C_CONV), lambda bb, t: (0, 0)), vec, vec, vec],
        out_specs=[pl.BlockSpec((1, ts, C_CONV), lambda bb, t: (bb, t, 0)),
                   pl.BlockSpec((1, CONV_W - 1, C_CONV), lambda bb, t: (bb, 0, 0))],
        out_shape=[jax.ShapeDtypeStruct((B, L, C_CONV), F32),
                   jax.ShapeDtypeStruct((B, CONV_W - 1, C_CONV), F32)],
        scratch_shapes=[pltpu.VMEM((HIST_ROWS + ts, C_CONV), F32)],
        compiler_params=_cparams("parallel", "arbitrary"),
        name="conv_branch",
    )(ag, prev32, w, cb.reshape(1, -1), g.reshape(1, -1), b.reshape(1, -1))


def _proj_ln_kernel(*refs, n_parts):
    x_ref = refs[0]
    parts = refs[1:1 + n_parts]
    ws = refs[1 + n_parts:1 + 2 * n_parts]
    g_ref, b_ref, o_ref = refs[1 + 2 * n_parts:]
    acc = _dot(parts[0][...], ws[0][...])
    for p, w in zip(parts[1:], ws[1:]):
        acc = acc + _dot(p[...], w[...])
    o_ref[...] = _ln(ALPHA * x_ref[...] + acc, g_ref[...], b_ref[...])


def proj_ln(x, parts, ws, g, b, tm):
    T = x.shape[0]
    row = lambda i: (i, 0)
    const = lambda i: (0, 0)
    return pl.pallas_call(
        functools.partial(_proj_ln_kernel, n_parts=len(parts)),
        grid=(T // tm,),
        in_specs=[pl.BlockSpec((tm, D_MODEL), row)]
        + [pl.BlockSpec((tm, p.shape[1]), row) for p in parts]
        + [pl.BlockSpec(w.shape, const) for w in ws]
        + [pl.BlockSpec((1, D_MODEL), const)] * 2,
        out_specs=pl.BlockSpec((tm, D_MODEL), row),
        out_shape=jax.ShapeDtypeStruct((T, D_MODEL), F32),
        compiler_params=_cparams("parallel"),
        name="proj_ln",
    )(x, *parts, *ws, g.reshape(1, -1), b.reshape(1, -1))


def _mm_kernel(x_ref, w_ref, o_ref):
    o_ref[...] = _dot(x_ref[...], w_ref[...]).astype(o_ref.dtype)


def matmul(x, w, tm, out_dtype=F32):
    M, K = x.shape
    N = w.shape[1]
    return pl.pallas_call(
        _mm_kernel,
        grid=(M // tm,),
        in_specs=[pl.BlockSpec((tm, K), lambda i: (i, 0)), pl.BlockSpec((K, N), lambda i: (0, 0))],
        out_specs=pl.BlockSpec((tm, N), lambda i: (i, 0)),
        out_shape=jax.ShapeDtypeStruct((M, N), out_dtype),
        compiler_params=_cparams("parallel"),
        name="matmul",
    )(x, w)


def _sgu_kernel(x_ref, win_ref, sg_ref, sb_ref, ws_ref, bs_ref, wout_ref, g_ref, b_ref, o_ref, vv_ref, gated):
    x = x_ref[...]
    tm = x.shape[0]
    z = _dot(x, win_ref[...])
    z = 0.5 * z * (1.0 + lax.erf(z * (2.0 ** -0.5)))
    u = z[:, 0:D_GM]
    v = _ln(z[:, D_GM:2 * D_GM], sg_ref[...], sb_ref[...])
    vv_ref[...] = v
    vb = v.astype(BF16)
    cg = D_GM // G_C
    for c in range(tm // GM_CHUNK):
        r0 = c * GM_CHUNK
        for gi in range(G_C):
            c0 = gi * cg
            mix = jnp.dot(ws_ref[gi], vb[r0:r0 + GM_CHUNK, c0:c0 + cg], preferred_element_type=F32)
            mix = mix + bs_ref[:, c0:c0 + cg]
            gated[r0:r0 + GM_CHUNK, c0:c0 + cg] = (u[r0:r0 + GM_CHUNK, c0:c0 + cg] * mix).astype(BF16)
    o_ref[...] = _ln(ALPHA * x + _dot(gated[...], wout_ref[...]), g_ref[...], b_ref[...])


def sgu_layer(x, w_in, sg, sb, ws, bs, w_out, g, b, tm):
    T = x.shape[0]
    row = lambda i: (i, 0)
    const = lambda i: (0, 0)
    vec = pl.BlockSpec((1, D_MODEL), const)
    return pl.pallas_call(
        _sgu_kernel,
        grid=(T // tm,),
        in_specs=[pl.BlockSpec((tm, D_MODEL), row), pl.BlockSpec(w_in.shape, const), vec, vec,
                  pl.BlockSpec(ws.shape, lambda i: (0, 0, 0)), pl.BlockSpec(bs.shape, const),
                  pl.BlockSpec(w_out.shape, const), vec, vec],
        out_specs=[pl.BlockSpec((tm, D_MODEL), row), pl.BlockSpec((tm, D_GM), row)],
        out_shape=[jax.ShapeDtypeStruct((T, D_MODEL), F32), jax.ShapeDtypeStruct((T, D_GM), F32)],
        scratch_shapes=[pltpu.VMEM((tm, D_GM), BF16)],
        compiler_params=_cparams("parallel"),
        name="sgu_layer",
    )(x, w_in, sg.reshape(1, -1), sb.reshape(1, -1), ws, bs, w_out, g.reshape(1, -1), b.reshape(1, -1))


def _cross_kernel(q_ref, k_ref, v_ref, o_ref):
    q = q_ref[0]
    k = k_ref[0]
    v = v_ref[0]
    for h in range(H_X):
        c0 = h * HD_X
        s = _dot_nt(q[:, c0:c0 + HD_X], k[:, c0:c0 + HD_X]) * (HD_X ** -0.5)
        m = jnp.max(s, axis=-1, keepdims=True)
        p = jnp.exp(s - m)
        l = jnp.sum(p, axis=-1, keepdims=True)
        o_ref[0, :, c0:c0 + HD_X] = _dot(p, v[:, c0:c0 + HD_X]) / l


def cross_core(q, k, v, layer, tq):
    B, L, _ = q.shape
    nm = k.shape[1]
    return pl.pallas_call(
        _cross_kernel,
        grid=(B, L // tq),
        in_specs=[pl.BlockSpec((1, tq, D_MODEL), lambda b, t: (b, t, 0)),
                  pl.BlockSpec((1, nm, D_MODEL), lambda b, t: (layer * B + b, 0, 0)),
                  pl.BlockSpec((1, nm, D_MODEL), lambda b, t: (layer * B + b, 0, 0))],
        out_specs=pl.BlockSpec((1, tq, D_MODEL), lambda b, t: (b, t, 0)),
        out_shape=jax.ShapeDtypeStruct((B, L, D_MODEL), F32),
        compiler_params=_cparams("parallel", "parallel"),
        name="cross_core",
    )(q, k, v)


MOE_ROWS = 256


def _route(x, wr, rb):
    lane = lax.broadcasted_iota(jnp.int32, (x.shape[0], LANES), 1)
    idx = lane % EPG
    scores = _sigmoid(_dot3(x, wr))
    sel = scores + rb

    def at(a, off):
        return pltpu.roll(a, (-off) % LANES, axis=1)

    others = []
    for d in range(1, EPG):
        inside = idx + d < EPG
        others.append((jnp.where(inside, at(sel, d), at(sel, d - EPG)), (idx + d) % EPG))
    a, b, c, dd = sel, others[0][0], others[1][0], others[2][0]
    hi1, lo1 = jnp.maximum(a, b), jnp.minimum(a, b)
    hi2, lo2 = jnp.maximum(c, dd), jnp.minimum(c, dd)
    gs = jnp.maximum(hi1, hi2) + jnp.maximum(jnp.minimum(hi1, hi2), jnp.maximum(lo1, lo2))
    best = lane < N_EXPERTS
    for off in range(EPG, N_EXPERTS, EPG):
        best = best & ((lane + off >= N_EXPERTS) | (gs >= at(gs, off)))
        best = best & ((lane - off < 0) | (gs > at(gs, -off)))
    rank = jnp.zeros(sel.shape, jnp.int32)
    for val, oidx in others:
        rank = rank + ((val > sel) | ((val == sel) & (oidx < idx))).astype(jnp.int32)
    chosen = best & (rank < 2)
    w = jnp.where(chosen, scores, 0.0)
    return w / jnp.sum(w, axis=1, keepdims=True)


def _moe_kernel(x_ref, wr_ref, rb_ref, wg_ref, wu_ref, wd_ref, g_ref, b_ref, o_ref, xb_s, comb_s, acc_s):
    e = pl.program_id(1)
    tm = x_ref.shape[0]
    rs = min(MOE_ROWS, tm)

    @pl.when(e == 0)
    def _():
        for r in range(0, tm, rs):
            x = x_ref[r:r + rs, :]
            comb_s[r:r + rs, :] = _route(x, wr_ref[...], rb_ref[...])
            xb_s[r:r + rs, :] = x.astype(BF16)
        acc_s[...] = jnp.zeros_like(acc_s)

    lane = lax.broadcasted_iota(jnp.int32, (rs, LANES), 1)
    for r in range(0, tm, rs):
        c = jnp.sum(jnp.where(lane == e, comb_s[r:r + rs, :], 0.0), axis=1, keepdims=True)
        xb = xb_s[r:r + rs, :]
        hg = jnp.dot(xb, wg_ref[0], preferred_element_type=F32)
        hu = jnp.dot(xb, wu_ref[0], preferred_element_type=F32)
        h = hg * _sigmoid(hg) * hu * c
        acc_s[r:r + rs, :] += _dot(h, wd_ref[0])

    @pl.when(e == pl.num_programs(1) - 1)
    def _():
        o_ref[...] = _ln(ALPHA * x_ref[...] + acc_s[...], g_ref[...], b_ref[...])


def moe_layer(x, wr, rb, wg, wu, wd, g, b, tm):
    T = x.shape[0]
    ne, _, dff = wg.shape
    row = lambda i, e: (i, 0)
    const = lambda i, e: (0, 0)
    return pl.pallas_call(
        _moe_kernel,
        grid=(T // tm, ne),
        in_specs=[pl.BlockSpec((tm, D_MODEL), row), pl.BlockSpec(wr.shape, const),
                  pl.BlockSpec((1, LANES), const),
                  pl.BlockSpec((1, D_MODEL, dff), lambda i, e: (e, 0, 0)),
                  pl.BlockSpec((1, D_MODEL, dff), lambda i, e: (e, 0, 0)),
                  pl.BlockSpec((1, dff, D_MODEL), lambda i, e: (e, 0, 0)),
                  pl.BlockSpec((1, D_MODEL), const), pl.BlockSpec((1, D_MODEL), const)],
        out_specs=pl.BlockSpec((tm, D_MODEL), row),
        out_shape=jax.ShapeDtypeStruct((T, D_MODEL), F32),
        scratch_shapes=[pltpu.VMEM((tm, D_MODEL), BF16), pltpu.VMEM((tm, LANES), F32),
                        pltpu.VMEM((tm, D_MODEL), F32)],
        compiler_params=_cparams("parallel", "arbitrary"),
        name="moe_layer",
    )(x, wr, rb, wg, wu, wd, g.reshape(1, -1), b.reshape(1, -1))


def _t5_bucket_np(rel):
    n = np.maximum(rel, 0)
    max_exact = N_BUCKETS // 2
    nf = np.maximum(n, 1).astype(np.float32)
    large = max_exact + (np.log(nf / np.float32(max_exact)) / np.float32(math.log(MAX_DISTANCE / max_exact))
                         * np.float32(N_BUCKETS - max_exact)).astype(np.int32)
    large = np.minimum(large, N_BUCKETS - 1)
    return np.where(n < max_exact, n, large)


def _bias_tables(rel_bias, past_len, ds):
    bias_t = rel_bias.T
    far = bias_t[:, N_BUCKETS - 1]
    kk = np.arange(MOBA_BLOCK)[:, None]
    qq = np.arange(MOBA_BLOCK)[None, :]
    b0 = bias_t[:, _t5_bucket_np(qq - kk)] - far[:, None, None]
    b1 = bias_t[:, _t5_bucket_np(MOBA_BLOCK + qq - kk)] - far[:, None, None]
    nb = past_len // MOBA_BLOCK
    r = np.arange(ds * H_A)
    rq, rh = r // H_A, r % H_A
    key = np.arange(MOBA_BLOCK)[None, :]
    rel_last = past_len + rq[:, None] - ((nb - 1) * MOBA_BLOCK + key)
    blast = bias_t[rh[:, None], _t5_bucket_np(rel_last)] - far[rh][:, None]
    j = np.arange(LANES)[None, :]
    rel_own = rq[:, None] - j
    ok = (rel_own >= 0) & (j < ds)
    bown = jnp.where(ok, bias_t[rh[:, None], _t5_bucket_np(rel_own)] - far[rh][:, None], NEG_INF)
    return b0.astype(F32), b1.astype(F32), blast.astype(F32), bown.astype(F32)


def kernel(x_prompt, x_sample, cache_k, cache_v, state_conv, cache_mem_k, cache_mem_v, page_table, mem_prompt, rel_bias, w_in_even, conv_w, conv_b, conv_ln_g, conv_ln_b, w_out_even, w_in_odd, sgu_ln_g, sgu_ln_b, sgu_w, sgu_b, w_out_odd, w_xq, w_xkv, w_xo, w_router, router_bias, w_exp_gate, w_exp_up, w_exp_down, ln_g, ln_b):
    B, S, _ = x_prompt.shape
    DB, DS, _ = x_sample.shape
    TP, TS_ = B * S, DB * DS
    n_even, n_phys = cache_k.shape[:2]
    n_pages = page_table.shape[1]
    past_len = n_pages * PAGE_SIZE
    depth = w_xq.shape[0]

    b0, b1, blast, bown = _bias_tables(rel_bias, past_len, DS)
    cache_k4 = cache_k.reshape(n_even, n_phys, PAGE_SIZE, D_A)
    cache_v4 = cache_v.reshape(n_even, n_phys, PAGE_SIZE, D_A)
    mem_k_s = cache_mem_k.reshape(depth * DB, -1, D_MODEL)
    mem_v_s = cache_mem_v.reshape(depth * DB, -1, D_MODEL)
    n_mem = mem_prompt.shape[1]
    mem2 = mem_prompt.reshape(B * n_mem, D_MODEL)
    wr = jnp.zeros((D_MODEL, LANES), F32).at[:, :N_EXPERTS].set(w_router)
    rb = jnp.zeros((1, LANES), F32).at[0, :N_EXPERTS].set(router_bias)
    causal = jnp.tril(jnp.ones((GM_CHUNK, GM_CHUNK), sgu_w.dtype))

    xp = x_prompt.reshape(TP, D_MODEL)
    xs = x_sample.reshape(TS_, D_MODEL)
    k_p, v_p, k_s, v_s, c_p, c_s, mk_p, mv_p, gv_s = ([] for _ in range(9))

    for l in range(depth):
        if l % 2 == 0:
            e = l // 2
            w_in = w_in_even[e].astype(BF16)
            w_att = w_out_even[e, :D_A].astype(BF16)
            w_cv = w_out_even[e, D_A:].astype(BF16)
            q, k, v, ag, km = even_in(xp, w_in, MOBA_BLOCK)
            att = moba_prompt(q, k, v, km.reshape(B, S // MOBA_BLOCK, D_A), b0, b1, B, S)
            cv, cst = conv_branch(ag.reshape(B, S, 2 * C_CONV), jnp.zeros((B, HIST_ROWS, C_CONV), F32),
                                  conv_w[e], conv_b[e], conv_ln_g[e], conv_ln_b[e], 512)
            xp = proj_ln(xp, [att, cv.reshape(TP, C_CONV)], [w_att, w_cv], ln_g[l, 0], ln_b[l, 0], 512)
            k_p.append(k.reshape(B, S, H_A, HD_A)); v_p.append(v.reshape(B, S, H_A, HD_A)); c_p.append(cst)
            q, k, v, ag, _ = even_in(xs, w_in, TS_)
            att = moba_sample(q.reshape(DB, DS, D_A), k.reshape(DB, DS, D_A), v.reshape(DB, DS, D_A),
                              cache_k4, cache_v4, e, page_table, blast, bown)
            prev32 = jnp.pad(state_conv[e], ((0, 0), (HIST_ROWS - (CONV_W - 1), 0), (0, 0)))
            cv, cst = conv_branch(ag.reshape(DB, DS, 2 * C_CONV), prev32,
                                  conv_w[e], conv_b[e], conv_ln_g[e], conv_ln_b[e], DS)
            xs = proj_ln(xs, [att.reshape(TS_, D_A), cv.reshape(TS_, C_CONV)], [w_att, w_cv],
                         ln_g[l, 0], ln_b[l, 0], TS_)
            k_s.append(k.reshape(DB, DS, H_A, HD_A)); v_s.append(v.reshape(DB, DS, H_A, HD_A)); c_s.append(cst)
        else:
            o = l // 2
            ws = sgu_w[o] * causal
            w_in = w_in_odd[o].astype(BF16)
            w_out = w_out_odd[o].astype(BF16)
            bs_p = jnp.repeat(sgu_b[o].T, D_GM // G_C, axis=1)
            xp, _ = sgu_layer(xp, w_in, sgu_ln_g[o], sgu_ln_b[o], ws.astype(BF16), bs_p, w_out,
                              ln_g[l, 0], ln_b[l, 0], 2 * GM_CHUNK)
            eye = jnp.eye(TS_ // DS, dtype=ws.dtype)
            ws_s = jnp.einsum('ab,gts->gatbs', eye, ws[:, :DS, :DS]).reshape(G_C, TS_, TS_)
            bs_s = jnp.tile(bs_p[:DS], (TS_ // DS, 1))
            xs, vv = sgu_layer(xs, w_in, sgu_ln_g[o], sgu_ln_b[o], ws_s.astype(BF16), bs_s, w_out,
                               ln_g[l, 0], ln_b[l, 0], TS_)
            gv_s.append(vv.reshape(DB, DS, D_GM))

        wq = w_xq[l].astype(BF16)
        wo = w_xo[l].astype(BF16)
        kv = matmul(mem2, w_xkv[l].astype(BF16), 512)
        mk, mv = kv[:, :D_MODEL], kv[:, D_MODEL:]
        mk_p.append(mk.reshape(B, n_mem, H_X, HD_X)); mv_p.append(mv.reshape(B, n_mem, H_X, HD_X))
        qx = matmul(xp, wq, 512)
        ox = cross_core(qx.reshape(B, S, D_MODEL), mk.reshape(B, n_mem, D_MODEL), mv.reshape(B, n_mem, D_MODEL), 0, 512)
        xp = proj_ln(xp, [ox.reshape(TP, D_MODEL)], [wo], ln_g[l, 1], ln_b[l, 1], 512)
        qx = matmul(xs, wq, TS_)
        ox = cross_core(qx.reshape(DB, DS, D_MODEL), mem_k_s, mem_v_s, l, DS)
        xs = proj_ln(xs, [ox.reshape(TS_, D_MODEL)], [wo], ln_g[l, 1], ln_b[l, 1], TS_)

        wg = w_exp_gate[l].astype(BF16)
        wu = w_exp_up[l].astype(BF16)
        wd = w_exp_down[l].astype(BF16)
        xp = moe_layer(xp, wr, rb, wg, wu, wd, ln_g[l, 2], ln_b[l, 2], 1024)
        xs = moe_layer(xs, wr, rb, wg, wu, wd, ln_g[l, 2], ln_b[l, 2], TS_)

    return (xp.reshape(B, S, D_MODEL), xs.reshape(DB, DS, D_MODEL),
            jnp.stack(k_p), jnp.stack(v_p), jnp.stack(k_s), jnp.stack(v_s),
            jnp.stack(c_p), jnp.stack(c_s), jnp.stack(mk_p), jnp.stack(mv_p), jnp.stack(gv_s))
```
